```python
import math
import jax, jax.numpy as jnp
from jax import lax
import numpy as np

D_MODEL = 1024
BATCH = 16
SEQ = 4096
DEPTH = 1

ATT_HEADS = 16
ATT_KV_HEADS = 2
ATT_HEAD_DIM = 64
WINDOW = 128
ROPE_DIM = ATT_HEAD_DIM // 4
ROPE_THETA = 500000.0
ML_HEADS = 8
ML_QK_DIM = 64
ML_V_DIM = 128
ML_CHUNK = 64
GATE_CAP = 15.0
PEER_HEADS = 8
PEER_KEY_DIM = 128
N_KEYS = 128
N_EXPERTS = N_KEYS * N_KEYS
PEER_TOPK = 16
PEER_TOKEN_BLOCK = 128
EPS = 1e-6

ATT_Q_W = ATT_HEADS * ATT_HEAD_DIM
ATT_KV_W = ATT_KV_HEADS * ATT_HEAD_DIM
ML_QK_W = ML_HEADS * ML_QK_DIM
ML_V_W = ML_HEADS * ML_V_DIM
IN_SPLITS = (ATT_Q_W, ATT_KV_W, ATT_KV_W, ML_QK_W, ML_QK_W, ML_V_W, ML_HEADS, ML_HEADS, ML_V_W, D_MODEL, D_MODEL)
IN_WIDTH = ATT_Q_W + 2 * ATT_KV_W + 2 * ML_QK_W + 2 * ML_V_W + 2 * ML_HEADS + 2 * D_MODEL

kernel_name = "hybrid_mlstm_swa_sink_peer"


def rmsnorm(x, gain):
    xf = x.astype(jnp.float32)
    r = lax.rsqrt(jnp.mean(xf * xf, axis=-1, keepdims=True) + EPS)
    return (xf * r * gain.astype(jnp.float32)).astype(x.dtype)


def partial_rope(x, positions):
    half = ROPE_DIM // 2
    inv_freq = ROPE_THETA ** (-jnp.arange(0, ROPE_DIM, 2, dtype=jnp.float32) / ROPE_DIM)
    ang = positions.astype(jnp.float32)[..., None] * inv_freq
    cos = jnp.cos(ang)[:, :, None, :]
    sin = jnp.sin(ang)[:, :, None, :]
    xf = x.astype(jnp.float32)
    x1 = xf[..., :half]
    x2 = xf[..., half:ROPE_DIM]
    rot = jnp.concatenate([x1 * cos - x2 * sin, x2 * cos + x1 * sin], axis=-1)
    return jnp.concatenate([rot, xf[..., ROPE_DIM:]], axis=-1).astype(x.dtype)


def sliding_window_attention(q, k, v, sinks):
    B, S = q.shape[0], q.shape[1]
    nb = S // WINDOW
    G = ATT_HEADS // ATT_KV_HEADS
    qb = q.reshape(B, nb, WINDOW, ATT_KV_HEADS, G, ATT_HEAD_DIM)
    kb = k.reshape(B, nb, WINDOW, ATT_KV_HEADS, ATT_HEAD_DIM)
    vb = v.reshape(B, nb, WINDOW, ATT_KV_HEADS, ATT_HEAD_DIM)

    def with_prev(t):
        prev = jnp.pad(t, ((0, 0), (1, 0), (0, 0), (0, 0), (0, 0)))[:, :-1]
        return jnp.concatenate([prev, t], axis=2)

    kband = with_prev(kb)
    vband = with_prev(vb)
    a_idx = jnp.arange(WINDOW)[:, None]
    c_idx = jnp.arange(2 * WINDOW)[None, :]
    band = (c_idx > a_idx) & (c_idx <= a_idx + WINDOW)
    sink = sinks.astype(jnp.float32).reshape(ATT_KV_HEADS, G)[None, :, :, None, None]
    scale = ATT_HEAD_DIM ** -0.5

    def one_block(args):
        qj, kj, vj, j = args
        s = jnp.einsum('bqhgd,bkhd->bhgqk', qj, kj).astype(jnp.float32) * scale
        valid = band & ((j * WINDOW + c_idx - WINDOW) >= 0)
        s = jnp.where(valid, s, -jnp.inf)
        m = jnp.maximum(jnp.max(s, axis=-1, keepdims=True), sink)
        p = jnp.exp(s - m)
        denom = jnp.sum(p, axis=-1, keepdims=True) + jnp.exp(sink - m)
        return jnp.einsum('bhgqk,bkhd->bqhgd', (p / denom).astype(vj.dtype), vj)

    xs = (jnp.moveaxis(qb, 1, 0), jnp.moveaxis(kband, 1, 0), jnp.moveaxis(vband, 1, 0),
          jnp.arange(nb, dtype=jnp.int32))
    out = lax.map(one_block, xs)
    return jnp.moveaxis(out, 0, 1).reshape(B, S, ATT_Q_W)


def mlstm_chunkwise(q, k, v, i_pre, f_pre):
    B, S, H = q.shape[0], q.shape[1], q.shape[2]
    L = ML_CHUNK
    nc = S // L
    f32 = jnp.float32
    qc = q.astype(f32).reshape(B, nc, L, H, ML_QK_DIM).transpose(1, 0, 3, 2, 4)
    kc = (k.astype(f32) * (ML_QK_DIM ** -0.5)).reshape(B, nc, L, H, ML_QK_DIM).transpose(1, 0, 3, 2, 4)
    vc = v.astype(f32).reshape(B, nc, L, H, ML_V_DIM).transpose(1, 0, 3, 2, 4)
    ic = i_pre.reshape(B, nc, L, H).transpose(1, 0, 3, 2)
    lfc = jax.nn.log_sigmoid(f_pre).reshape(B, nc, L, H).transpose(1, 0, 3, 2)
    causal = jnp.tril(jnp.ones((L, L), dtype=bool))

    def step(carry, xs):
        C, n, m = carry
        qj, kj, vj, ij, lfj = xs
        b = jnp.cumsum(lfj, axis=-1)
        dmat = jnp.where(causal, b[..., :, None] - b[..., None, :] + ij[..., None, :], -jnp.inf)
        inter = b + m[..., None]
        m_t = jnp.maximum(inter, jnp.max(dmat, axis=-1))
        w_intra = jnp.exp(dmat - m_t[..., None])
        w_inter = jnp.exp(inter - m_t)
        qk = jnp.einsum('bhtd,bhsd->bhts', qj, kj) * w_intra
        num = jnp.einsum('bhts,bhsv->bhtv', qk, vj) + w_inter[..., None] * jnp.einsum('bhvd,bhtd->bhtv', C, qj)
        den = jnp.sum(qk, axis=-1) + w_inter * jnp.einsum('bhd,bhtd->bht', n, qj)
        h = num / jnp.maximum(jnp.abs(den), jnp.exp(-m_t))[..., None]
        b_last = b[..., -1]
        g = b_last[..., None] - b + ij
        m_new = jnp.maximum(b_last + m, jnp.max(g, axis=-1))
        w_s = jnp.exp(g - m_new[..., None])
        decay = jnp.exp(b_last + m - m_new)
        C_new = decay[..., None, None] * C + jnp.einsum('bhs,bhsv,bhsd->bhvd', w_s, vj, kj)
        n_new = decay[..., None] * n + jnp.einsum('bhs,bhsd->bhd', w_s, kj)
        return (C_new, n_new, m_new), h

    init = (jnp.zeros((B, H, ML_V_DIM, ML_QK_DIM), f32), jnp.zeros((B, H, ML_QK_DIM), f32),
            jnp.zeros((B, H), f32))
    _, h = lax.scan(step, init, (qc, kc, vc, ic, lfc))
    return h.transpose(1, 0, 3, 2, 4).reshape(B, S, H, ML_V_DIM).astype(q.dtype)


def peer_ffn(x, w_query, sub_keys, expert_down, expert_up):
    B, S, D = x.shape
    xt = x.reshape((B * S) // PEER_TOKEN_BLOCK, PEER_TOKEN_BLOCK, D)
    K = PEER_TOPK

    def one_block(xb):
        t = xb.shape[0]
        q = (xb @ w_query).reshape(t, PEER_HEADS, 2, PEER_KEY_DIM // 2)
        s = jnp.einsum('thpd,hpkd->thpk', q, sub_keys).astype(jnp.float32)
        top_s, top_i = lax.top_k(s, K)
        cand_s = (top_s[:, :, 0, :, None] + top_s[:, :, 1, None, :]).reshape(t, PEER_HEADS, K * K)
        cand_i = (top_i[:, :, 0, :, None] * N_KEYS + top_i[:, :, 1, None, :]).reshape(t, PEER_HEADS, K * K)
        best_s, best_pos = lax.top_k(cand_s, K)
        idx = jnp.take_along_axis(cand_i, best_pos, axis=-1)
        gate = jax.nn.softmax(best_s, axis=-1)
        u = expert_down[idx]
        act = jax.nn.gelu(jnp.einsum('td,thkd->thk', xb, u).astype(jnp.float32), approximate=False)
        vv = expert_up[idx]
        return jnp.einsum('thk,thkd->td', (gate * act).astype(xb.dtype), vv)

    return lax.map(one_block, xt).reshape(B, S, D)


def split_columns(proj):
    offs = np.cumsum(np.array(IN_SPLITS))[:-1].tolist()
    return jnp.split(proj, offs, axis=-1)


def setup_inputs(seed: int = 0) -> dict:
    key = jax.random.key(seed)
    ks = jax.random.split(key, 20)
    f32 = jnp.float32
    nrm = lambda k, shape, s: jax.random.normal(k, shape, f32) * s
    x = jax.random.normal(ks[0], (BATCH, SEQ, D_MODEL), f32)
    offset = jax.random.randint(ks[1], (BATCH, 1), 0, 1024, dtype=jnp.int32)
    positions = offset + jnp.arange(SEQ, dtype=jnp.int32)[None, :]
    return {
        "x": x,
        "positions": positions,
        "norm1_gain": 1.0 + nrm(ks[2], (DEPTH, D_MODEL), 0.02),
        "w_in": nrm(ks[3], (DEPTH, D_MODEL, IN_WIDTH), D_MODEL ** -0.5),
        "ml_i_bias": nrm(ks[4], (DEPTH, ML_HEADS), 0.1),
        "ml_f_bias": jnp.linspace(3.0, 6.0, ML_HEADS, dtype=f32)[None, :] + nrm(ks[5], (DEPTH, ML_HEADS), 0.1),
        "q_norm_gain": 1.0 + nrm(ks[6], (DEPTH, ATT_HEAD_DIM), 0.02),
        "k_norm_gain": 1.0 + nrm(ks[7], (DEPTH, ATT_HEAD_DIM), 0.02),
        "attn_sinks": nrm(ks[8], (DEPTH, ATT_HEADS), 0.5),
        "ml_out_norm_gain": 1.0 + nrm(ks[9], (DEPTH, ML_V_W), 0.02),
        "w_branch_attn": nrm(ks[10], (DEPTH, ATT_Q_W, D_MODEL), ATT_Q_W ** -0.5),
        "w_branch_mlstm": nrm(ks[11], (DEPTH, ML_V_W, D_MODEL), ML_V_W ** -0.5),
        "w_out": nrm(ks[12], (DEPTH, D_MODEL, D_MODEL), D_MODEL ** -0.5),
        "norm2_gain": 1.0 + nrm(ks[13], (DEPTH, D_MODEL), 0.02),
        "peer_w_query": nrm(ks[14], (DEPTH, D_MODEL, PEER_HEADS * PEER_KEY_DIM), D_MODEL ** -0.5),
        "peer_sub_keys": nrm(ks[15], (DEPTH, PEER_HEADS, 2, N_KEYS, PEER_KEY_DIM // 2), (PEER_KEY_DIM // 2) ** -0.5),
        "peer_down": nrm(ks[16], (DEPTH, N_EXPERTS, D_MODEL), D_MODEL ** -0.5),
        "peer_up": nrm(ks[17], (DEPTH, N_EXPERTS, D_MODEL), PEER_HEADS ** -0.5),
    }


def reference(x, positions, norm1_gain, w_in, ml_i_bias, ml_f_bias, q_norm_gain, k_norm_gain,
              attn_sinks, ml_out_norm_gain, w_branch_attn, w_branch_mlstm, w_out, norm2_gain,
              peer_w_query, peer_sub_keys, peer_down, peer_up):
    B, S, _ = x.shape
    for l in range(DEPTH):
        h = rmsnorm(x, norm1_gain[l])
        (aq, ak, av, mq, mk, mv, mi, mf, mo, gate_a, gate_m) = split_columns(h @ w_in[l])
        aq = partial_rope(rmsnorm(aq.reshape(B, S, ATT_HEADS, ATT_HEAD_DIM), q_norm_gain[l]), positions)
        ak = partial_rope(rmsnorm(ak.reshape(B, S, ATT_KV_HEADS, ATT_HEAD_DIM), k_norm_gain[l]), positions)
        av = av.reshape(B, S, ATT_KV_HEADS, ATT_HEAD_DIM)
        att = sliding_window_attention(aq, ak, av, attn_sinks[l])
        att_b = att @ w_branch_attn[l]
        i_pre = mi.astype(jnp.float32) + ml_i_bias[l].astype(jnp.float32)
        f_pre = mf.astype(jnp.float32) + ml_f_bias[l].astype(jnp.float32)
        i_pre = GATE_CAP * jnp.tanh(i_pre / GATE_CAP)
        f_pre = GATE_CAP * jnp.tanh(f_pre / GATE_CAP)
        hm = mlstm_chunkwise(mq.reshape(B, S, ML_HEADS, ML_QK_DIM), mk.reshape(B, S, ML_HEADS, ML_QK_DIM),
                             mv.reshape(B, S, ML_HEADS, ML_V_DIM), i_pre, f_pre)
        hm = rmsnorm(hm, ml_out_norm_gain[l].reshape(ML_HEADS, ML_V_DIM)).reshape(B, S, ML_V_W)
        hm = hm * jax.nn.sigmoid(mo)
        ml_b = hm @ w_branch_mlstm[l]
        mixed = jax.nn.sigmoid(gate_a) * att_b + jax.nn.sigmoid(gate_m) * ml_b
        x = x + mixed @ w_out[l]
        x = x + peer_ffn(rmsnorm(x, norm2_gain[l]), peer_w_query[l], peer_sub_keys[l], peer_down[l], peer_up[l])
    return x
```

```python
import functools

import jax
import jax.numpy as jnp
import numpy as np
from jax import lax
from jax.experimental import pallas as pl
from jax.experimental.pallas import tpu as pltpu

F32 = jnp.float32
BF16 = jnp.bfloat16

NORM_EPS = 1e-6
D_MODEL_ = 1024
ATT_HEADS_ = 16
ATT_KV_HEADS_ = 2
ATT_DIM = 64
ATT_WINDOW = 128
ROPE_DIMS = 16
ROPE_BASE = 500000.0
ML_HEADS_ = 8
ML_QK = 64
ML_V = 128
GATE_SOFTCAP = 15.0
PEER_HEADS_ = 8
PEER_SUBKEY_DIM = 64
PEER_KEYS = 128
PEER_K = 16

LANES = 128
VMEM_LIMIT = 56 * 1024 * 1024

PROJ_TOKENS = 512
ML_CHUNK_LEN = 128
ROUTE_TOKENS = 256
PEER_TOKENS = 1024
PEER_PAIRS_PER_STEP = 4


def _params(*sem):
    return pltpu.CompilerParams(dimension_semantics=sem, vmem_limit_bytes=VMEM_LIMIT)


def _rms_rows(x, gain):
    r = lax.rsqrt(jnp.mean(x * x, axis=-1, keepdims=True) + NORM_EPS)
    return x * r * gain


def _trig_kernel(pos_ref, freq_ref, cos_ref, sin_ref):
    pos = pos_ref[...].astype(F32)
    for k in range(ROPE_DIMS // 2):
        ang = pos * freq_ref[k]
        cos_ref[k] = jnp.cos(ang)
        sin_ref[k] = jnp.sin(ang)


def _rope_tables(positions):
    t = positions.size
    half = ROPE_DIMS // 2
    inv_freq = ROPE_BASE ** (-jnp.arange(0, ROPE_DIMS, 2, dtype=F32) / ROPE_DIMS)
    pos2 = positions.reshape(t // LANES, LANES)
    rows = t // LANES
    cos, sin = pl.pallas_call(
        _trig_kernel,
        name="rope_trig",
        grid=(1,),
        in_specs=[pl.BlockSpec((rows, LANES), lambda i: (0, 0)),
                  pl.BlockSpec(memory_space=pltpu.SMEM)],
        out_specs=[pl.BlockSpec((half, rows, LANES), lambda i: (0, 0, 0))] * 2,
        out_shape=[jax.ShapeDtypeStruct((half, rows, LANES), F32)] * 2,
        compiler_params=_params("arbitrary"),
    )(pos2, inv_freq)
    cos = cos.reshape(half, t).T
    sin = sin.reshape(half, t).T
    ones = jnp.ones((t, ATT_DIM - ROPE_DIMS), F32)
    zeros8 = jnp.zeros((t, half), F32)
    zeros48 = jnp.zeros((t, ATT_DIM - ROPE_DIMS), F32)
    c_tab = jnp.concatenate([cos, cos, ones], axis=1)
    s_up = jnp.concatenate([-sin, zeros8, zeros48], axis=1)
    s_dn = jnp.concatenate([zeros8, sin, zeros48], axis=1)
    rep = LANES // ATT_DIM
    return tuple(jnp.tile(a, (1, rep)) for a in (c_tab, s_up, s_dn))


def _head_norm_rope(t, bsum, gain, c_tab, s_up, s_dn, scale):
    ms = jnp.dot((t * t).astype(BF16), bsum, preferred_element_type=F32)
    tn = t * lax.rsqrt(ms + NORM_EPS) * gain
    half = ROPE_DIMS // 2
    up = pltpu.roll(tn, LANES - half, axis=1)
    dn = pltpu.roll(tn, half, axis=1)
    return (tn * c_tab + up * s_up + dn * s_dn) * scale


def _attn_proj_kernel(x_ref, g1_ref, w_ref, bsum_ref, gq_ref, gk_ref, c_ref, su_ref, sd_ref,
                      q_out, k_out, v_out):
    h = _rms_rows(x_ref[...], g1_ref[...]).astype(BF16)
    p = jnp.dot(h, w_ref[...], preferred_element_type=F32)
    bsum = bsum_ref[...]
    c_tab, s_up, s_dn = c_ref[...], su_ref[...], sd_ref[...]
    qw = ATT_HEADS_ * ATT_DIM
    for c in range(qw // LANES):
        sl = slice(c * LANES, (c + 1) * LANES)
        q_out[:, sl] = _head_norm_rope(p[:, sl], bsum, gq_ref[...], c_tab, s_up, s_dn,
                                       ATT_DIM ** -0.5).astype(BF16)
    kw = ATT_KV_HEADS_ * ATT_DIM
    k_out[...] = _head_norm_rope(p[:, qw:qw + kw], bsum, gk_ref[...], c_tab, s_up, s_dn,
                                 1.0).astype(BF16)
    v_out[...] = p[:, qw + kw:qw + 2 * kw].astype(BF16)


def _soft_gates(z, is_input_gate):
    capped = GATE_SOFTCAP * jnp.tanh(z / GATE_SOFTCAP)
    return jnp.where(is_input_gate, capped, jax.nn.log_sigmoid(capped))


def _mlstm_proj_kernel(x_ref, g1_ref, wq_ref, wkT_ref, wv_ref, wg_ref, wgT_ref, bg_ref, bgT_ref,
                       q_out, kT_out, v_out, gcol_out, grow_out):
    h = _rms_rows(x_ref[...], g1_ref[...]).astype(BF16)
    q_out[...] = jnp.dot(h, wq_ref[...], preferred_element_type=F32).astype(BF16)
    v_out[...] = jnp.dot(h, wv_ref[...], preferred_element_type=F32).astype(BF16)
    nt = (((1,), (1,)), ((), ()))
    kT = lax.dot_general(wkT_ref[...], h, nt, preferred_element_type=F32)
    kT_out[...] = (kT * (ML_QK ** -0.5)).astype(BF16)
    zc = jnp.dot(h, wg_ref[...], preferred_element_type=F32) + bg_ref[...]
    col = lax.broadcasted_iota(jnp.int32, zc.shape, 1)
    gcol_out[...] = _soft_gates(zc, col < ML_HEADS_)
    zr = lax.dot_general(wgT_ref[...], h, nt, preferred_element_type=F32) + bgT_ref[...]
    row = lax.broadcasted_iota(jnp.int32, zr.shape, 0)
    grow_out[...] = _soft_gates(zr, row < ML_HEADS_)


def _gate_proj_kernel(x_ref, g1_ref, w_ref, o_out, ga_out, gm_out):
    h = _rms_rows(x_ref[...], g1_ref[...]).astype(BF16)
    d = D_MODEL_
    for i, out in enumerate((o_out, ga_out, gm_out)):
        z = jnp.dot(h, w_ref[:, i * d:(i + 1) * d], preferred_element_type=F32)
        out[...] = jax.nn.sigmoid(z).astype(BF16)


def _tok_spec(tm, width):
    return pl.BlockSpec((tm, width), lambda i: (i, 0))


def _whole(shape):
    nd = len(shape)
    return pl.BlockSpec(shape, lambda i: (0,) * nd)


def _attn_kernel(sink_ref, q_ref, kp_ref, kc_ref, vp_ref, vc_ref, o_ref):
    j = pl.program_id(1)
    w = ATT_WINDOW
    kk = jnp.concatenate([kp_ref[...], kc_ref[...]], axis=0)
    vv = jnp.concatenate([vp_ref[...], vc_ref[...]], axis=0)
    a = lax.broadcasted_iota(jnp.int32, (w, 2 * w), 0)
    c = lax.broadcasted_iota(jnp.int32, (w, 2 * w), 1)
    valid = (c > a) & (c <= a + w) & ((c >= w) | (j > 0))
    group = ATT_HEADS_ // ATT_KV_HEADS_
    nt = (((1,), (1,)), ((), ()))
    for h in range(ATT_HEADS_):
        g = h // group
        qh = q_ref[:, h * ATT_DIM:(h + 1) * ATT_DIM]
        kh = kk[:, g * ATT_DIM:(g + 1) * ATT_DIM]
        vh = vv[:, g * ATT_DIM:(g + 1) * ATT_DIM]
        s = lax.dot_general(qh, kh, nt, preferred_element_type=F32)
        s = jnp.where(valid, s, -jnp.inf)
        sink = sink_ref[h]
        m = jnp.maximum(jnp.max(s, axis=-1, keepdims=True), sink)
        p = jnp.exp(s - m)
        denom = jnp.sum(p, axis=-1, keepdims=True) + jnp.exp(sink - m)
        ph = (p / denom).astype(BF16)
        o_ref[:, h * ATT_DIM:(h + 1) * ATT_DIM] = jnp.dot(
            ph, vh, preferred_element_type=F32).astype(BF16)


def _mlstm_kernel(q_ref, kT_ref, v_ref, gcol_ref, grow_ref, og_ref, gain_ref, o_ref,
                  state_ref, m_ref):
    c = pl.program_id(1)
    L = ML_CHUNK_LEN

    @pl.when(c == 0)
    def _():
        state_ref[...] = jnp.zeros_like(state_ref)
        m_ref[...] = jnp.zeros_like(m_ref)

    t_idx = lax.broadcasted_iota(jnp.int32, (L, L), 0)
    s_idx = lax.broadcasted_iota(jnp.int32, (L, L), 1)
    causal = s_idx <= t_idx
    tri = causal.astype(F32)
    gcol = gcol_ref[...]
    grow = grow_ref[...]
    hi = lax.Precision.HIGHEST
    bcol = jnp.dot(tri, gcol, precision=hi, preferred_element_type=F32)
    brow = lax.dot_general(grow, tri, (((1,), (1,)), ((), ())), precision=hi,
                           preferred_element_type=F32)
    ones_col = (lax.broadcasted_iota(jnp.int32, (L, ML_V), 1) == 0).astype(BF16)
    nh = ML_HEADS_
    for h in range(nh):
        qh = q_ref[:, h * ML_QK:(h + 1) * ML_QK]
        kTh = kT_ref[h * ML_QK:(h + 1) * ML_QK, :]
        v_aug = jnp.concatenate([v_ref[:, h * ML_V:(h + 1) * ML_V], ones_col], axis=1)
        b_c = bcol[:, nh + h:nh + h + 1]
        b_r = brow[nh + h:nh + h + 1, :]
        i_r = grow[h:h + 1, :]
        m_prev = m_ref[h]
        state = state_ref[h]

        dmat = jnp.where(causal, b_c - b_r + i_r, -jnp.inf)
        inter = b_c + m_prev
        m_t = jnp.maximum(inter, jnp.max(dmat, axis=-1, keepdims=True))
        w_intra = jnp.exp(dmat - m_t)
        w_inter = jnp.exp(inter - m_t)
        qk = jnp.dot(qh, kTh, preferred_element_type=F32) * w_intra
        acc = jnp.dot(qk.astype(BF16), v_aug, preferred_element_type=F32)
        acc = acc + w_inter * jnp.dot(qh, state.astype(BF16), preferred_element_type=F32)
        num = acc[:, :ML_V]
        den = acc[:, ML_V:ML_V + 1]
        hh = num / jnp.maximum(jnp.abs(den), jnp.exp(-m_t))

        b_last = b_r[:, L - 1:L]
        g_r = b_last - b_r + i_r
        m_new = jnp.maximum(b_last + m_prev, jnp.max(g_r, axis=-1, keepdims=True))
        w_s = jnp.exp(g_r - m_new)
        decay = jnp.exp(b_last + m_prev - m_new)
        kw = (kTh.astype(F32) * w_s).astype(BF16)
        state_ref[h] = decay * state + jnp.dot(kw, v_aug, preferred_element_type=F32)
        m_ref[h] = m_new

        sl = slice(h * ML_V, (h + 1) * ML_V)
        hn = _rms_rows(hh, gain_ref[:, sl])
        o_ref[:, sl] = (hn * og_ref[:, sl].astype(F32)).astype(BF16)


def _merge_kernel(x_ref, att_ref, hm_ref, ga_ref, gm_ref, wa_ref, wm_ref, wo_ref, o_ref):
    att_b = jnp.dot(att_ref[...], wa_ref[...], preferred_element_type=F32)
    ml_b = jnp.dot(hm_ref[...], wm_ref[...], preferred_element_type=F32)
    mixed = ga_ref[...].astype(F32) * att_b + gm_ref[...].astype(F32) * ml_b
    o_ref[...] = x_ref[...] + jnp.dot(mixed.astype(BF16), wo_ref[...], preferred_element_type=F32)


def _top_values(s, count):
    rows = []
    cur = s
    for _ in range(count):
        m = jnp.max(cur, axis=0, keepdims=True)
        rows.append(m)
        cur = jnp.where(cur == m, -jnp.inf, cur)
    return jnp.concatenate(rows, axis=0)


def _route_kernel(x_ref, g2_ref, wqT_ref, keys_ref, xnT_out, rank_out, e1_out, nkey_out, f_out):
    xn = _rms_rows(x_ref[...], g2_ref[...])
    xnT = xn.T.astype(BF16)
    xnT_out[...] = xnT
    qT = jnp.dot(wqT_ref[...], xnT, preferred_element_type=F32).astype(BF16)
    k = PEER_K
    for h in range(PEER_HEADS_):
        s = []
        v = []
        for p in range(2):
            r0 = (h * 2 + p) * PEER_SUBKEY_DIM
            sp = jnp.dot(keys_ref[h * 2 + p], qT[r0:r0 + PEER_SUBKEY_DIM, :],
                         preferred_element_type=F32)
            s.append(sp)
            v.append(_top_values(sp, k))
        s0, s1 = s
        v0, v1 = v
        cand = [v0[0:1] + v1]
        cand += [v0[a:a + 1] + v1[0:8] for a in range(1, 8)]
        cand += [v0[8:16] + v1[0:1]]
        cand = jnp.concatenate(cand, axis=0)
        theta = _top_values(cand, k)[k - 1:k]
        top = v0[0:1] + v1[0:1]
        z = jnp.sum(jnp.where(cand >= theta, jnp.exp(cand - top), 0.0), axis=0, keepdims=True)
        nkey = jnp.zeros_like(s0)
        rank = jnp.zeros_like(s1)
        for b in range(k):
            nkey = nkey + jnp.where(s0 + v1[b:b + 1] >= theta, 1.0, 0.0)
            rank = rank + jnp.where(v1[b:b + 1] > s1, 1.0, 0.0)
        rank_out[h] = rank.astype(BF16)
        e1_out[h] = jnp.exp(s1 - v1[0:1]).astype(BF16)
        nkey_out[h] = nkey
        f_out[h] = jnp.exp(s0 - v0[0:1]) / z


def _peer_kernel(xnT_ref, down_ref, upT_ref, rank_ref, e1_ref, nkey_ref, f_ref, x_ref, o_ref,
                 acc_ref):
    st = pl.program_id(1)

    @pl.when(st == 0)
    def _():
        acc_ref[...] = jnp.zeros_like(acc_ref)

    xnT = xnT_ref[...]
    blk = PEER_KEYS
    for pr in range(PEER_PAIRS_PER_STEP):
        rows = slice(pr * 2 * blk, (pr + 1) * 2 * blk)
        s = jnp.dot(down_ref[rows, :], xnT, preferred_element_type=F32)
        act = (0.5 * s * (1.0 + lax.erf(s * np.float32(np.sqrt(0.5))))).astype(BF16)
        gates = []
        for r in range(2):
            row = pr * 2 + r
            g = None
            for h in range(PEER_HEADS_):
                n = nkey_ref[h, row:row + 1, :].astype(BF16)
                fr = f_ref[h, row:row + 1, :].astype(BF16)
                term = jnp.where(rank_ref[h] < n, e1_ref[h], jnp.zeros((), BF16)) * fr
                g = term if g is None else g + term
            gates.append(g)
        w = jnp.concatenate(gates, axis=0) * act
        acc_ref[...] += jnp.dot(upT_ref[:, rows], w, preferred_element_type=F32)

    @pl.when(st == pl.num_programs(1) - 1)
    def _():
        o_ref[...] = x_ref[...] + acc_ref[...].T


def _token_mixer(x, positions, norm1_gain, w_in, ml_i_bias, ml_f_bias, q_norm_gain, k_norm_gain,
                 attn_sinks, ml_out_norm_gain, w_branch_attn, w_branch_mlstm, w_out):
    B, S, D = x.shape
    assert D == D_MODEL_ and norm1_gain.shape[0] == 1
    T = B * S
    tm = min(PROJ_TOKENS, T)
    assert T % tm == 0 and S % ATT_WINDOW == 0 and S % ML_CHUNK_LEN == 0
    x2 = x.reshape(T, D)
    g1 = norm1_gain[0].reshape(1, D)

    qw = ATT_HEADS_ * ATT_DIM
    kvw = ATT_KV_HEADS_ * ATT_DIM
    mqk = ML_HEADS_ * ML_QK
    mvw = ML_HEADS_ * ML_V
    nh = ML_HEADS_
    o0 = 0
    offs = {}
    for name, width in (("aq", qw), ("ak", kvw), ("av", kvw), ("mq", mqk), ("mk", mqk), ("mv", mvw),
                        ("mi", nh), ("mf", nh), ("mo", mvw), ("ga", D), ("gm", D)):
        offs[name] = (o0, o0 + width)
        o0 += width
    w = w_in[0]
    col = lambda a, b=None: w[:, offs[a][0]:offs[b or a][1]]

    c_tab, s_up, s_dn = _rope_tables(positions)
    seg = np.arange(qw) // ATT_DIM
    bsum = jnp.asarray((seg[:, None] == seg[None, :]).astype(np.float32) / ATT_DIM, BF16)
    bsum = bsum[:LANES, :LANES]
    rep = LANES // ATT_DIM
    gq = jnp.tile(q_norm_gain[0], rep).reshape(1, LANES)
    gk = jnp.tile(k_norm_gain[0], rep).reshape(1, LANES)
    w_att = col("aq", "av").astype(BF16)
    grid_t = (T // tm,)
    q, k, v = pl.pallas_call(
        _attn_proj_kernel,
        name="attn_proj",
        grid=grid_t,
        in_specs=[_tok_spec(tm, D), _whole((1, D)), _whole(w_att.shape), _whole(bsum.shape),
                  _whole((1, LANES)), _whole((1, LANES)),
                  _tok_spec(tm, LANES), _tok_spec(tm, LANES), _tok_spec(tm, LANES)],
        out_specs=[_tok_spec(tm, qw), _tok_spec(tm, kvw), _tok_spec(tm, kvw)],
        out_shape=[jax.ShapeDtypeStruct((T, qw), BF16), jax.ShapeDtypeStruct((T, kvw), BF16),
                   jax.ShapeDtypeStruct((T, kvw), BF16)],
        compiler_params=_params("parallel"),
    )(x2, g1, w_att, bsum, gq, gk, c_tab, s_up, s_dn)

    wg = jnp.zeros((D, LANES), F32).at[:, :2 * nh].set(col("mi", "mf")).astype(BF16)
    bg = jnp.zeros((1, LANES), F32).at[0, :nh].set(ml_i_bias[0]).at[0, nh:2 * nh].set(ml_f_bias[0])
    wgT = wg[:, :2 * nh].T
    bgT = bg[:, :2 * nh].T
    mq, mkT, mv, gcol, grow = pl.pallas_call(
        _mlstm_proj_kernel,
        name="mlstm_proj",
        grid=grid_t,
        in_specs=[_tok_spec(tm, D), _whole((1, D)), _whole((D, mqk)), _whole((mqk, D)),
                  _whole((D, mvw)), _whole((D, LANES)), _whole((2 * nh, D)), _whole((1, LANES)),
                  _whole((2 * nh, 1))],
        out_specs=[_tok_spec(tm, mqk), pl.BlockSpec((mqk, tm), lambda i: (0, i)),
                   _tok_spec(tm, mvw), _tok_spec(tm, LANES),
                   pl.BlockSpec((2 * nh, tm), lambda i: (0, i))],
        out_shape=[jax.ShapeDtypeStruct((T, mqk), BF16), jax.ShapeDtypeStruct((mqk, T), BF16),
                   jax.ShapeDtypeStruct((T, mvw), BF16), jax.ShapeDtypeStruct((T, LANES), F32),
                   jax.ShapeDtypeStruct((2 * nh, T), F32)],
        compiler_params=_params("parallel"),
    )(x2, g1, col("mq").astype(BF16), col("mk").T.astype(BF16), col("mv").astype(BF16),
      wg, wgT, bg, bgT)

    w_g3 = col("mo", "gm").astype(BF16)
    og, ga, gm = pl.pallas_call(
        _gate_proj_kernel,
        name="gate_proj",
        grid=grid_t,
        in_specs=[_tok_spec(tm, D), _whole((1, D)), _whole(w_g3.shape)],
        out_specs=[_tok_spec(tm, D)] * 3,
        out_shape=[jax.ShapeDtypeStruct((T, D), BF16)] * 3,
        compiler_params=_params("parallel"),
    )(x2, g1, w_g3)

    nb = S // ATT_WINDOW
    cur = lambda b, j: (b * nb + j, 0)
    prev = lambda b, j: (b * nb + jnp.maximum(j - 1, 0), 0)
    att = pl.pallas_call(
        _attn_kernel,
        name="swa_attention",
        grid=(B, nb),
        in_specs=[pl.BlockSpec(memory_space=pltpu.SMEM),
                  pl.BlockSpec((ATT_WINDOW, qw), cur),
                  pl.BlockSpec((ATT_WINDOW, kvw), prev), pl.BlockSpec((ATT_WINDOW, kvw), cur),
                  pl.BlockSpec((ATT_WINDOW, kvw), prev), pl.BlockSpec((ATT_WINDOW, kvw), cur)],
        out_specs=pl.BlockSpec((ATT_WINDOW, qw), cur),
        out_shape=jax.ShapeDtypeStruct((T, qw), BF16),
        compiler_params=_params("parallel", "parallel"),
    )(attn_sinks[0], q, k, k, v, v)

    L = ML_CHUNK_LEN
    nc = S // L
    tok = lambda b, c: (b * nc + c, 0)
    tokT = lambda b, c: (0, b * nc + c)
    hm = pl.pallas_call(
        _mlstm_kernel,
        name="mlstm_chunks",
        grid=(B, nc),
        in_specs=[pl.BlockSpec((L, mqk), tok), pl.BlockSpec((mqk, L), tokT),
                  pl.BlockSpec((L, mvw), tok), pl.BlockSpec((L, LANES), tok),
                  pl.BlockSpec((2 * nh, L), tokT), pl.BlockSpec((L, mvw), tok),
                  pl.BlockSpec((1, mvw), lambda b, c: (0, 0))],
        out_specs=pl.BlockSpec((L, mvw), tok),
        out_shape=jax.ShapeDtypeStruct((T, mvw), BF16),
        scratch_shapes=[pltpu.VMEM((nh, ML_QK, 2 * ML_V), F32), pltpu.VMEM((nh, 1, 1), F32)],
        compiler_params=_params("parallel", "arbitrary"),
    )(mq, mkT, mv, gcol, grow, og, ml_out_norm_gain[0].reshape(1, mvw))

    x1 = pl.pallas_call(
        _merge_kernel,
        name="gated_merge",
        grid=grid_t,
        in_specs=[_tok_spec(tm, D), _tok_spec(tm, qw), _tok_spec(tm, mvw), _tok_spec(tm, D),
                  _tok_spec(tm, D), _whole((qw, D)), _whole((mvw, D)), _whole((D, D))],
        out_specs=_tok_spec(tm, D),
        out_shape=jax.ShapeDtypeStruct((T, D), F32),
        compiler_params=_params("parallel"),
    )(x2, att, hm, ga, gm, w_branch_attn[0].astype(BF16), w_branch_mlstm[0].astype(BF16),
      w_out[0].astype(BF16))
    return x1


def _peer_ffn(x1, norm2_gain, peer_w_query, peer_sub_keys, peer_down, peer_up):
    T, D = x1.shape
    tr = min(ROUTE_TOKENS, T)
    ph = PEER_HEADS_
    nk = PEER_KEYS
    keys = peer_sub_keys[0].reshape(ph * 2, nk, PEER_SUBKEY_DIM).astype(BF16)
    head_tok = pl.BlockSpec((ph, nk, tr), lambda i: (0, 0, i))
    xnT, rank, e1, nkey, fgate = pl.pallas_call(
        _route_kernel,
        name="peer_route",
        grid=(T // tr,),
        in_specs=[_tok_spec(tr, D), _whole((1, D)), _whole((ph * 2 * PEER_SUBKEY_DIM, D)),
                  _whole(keys.shape)],
        out_specs=[pl.BlockSpec((D, tr), lambda i: (0, i)), head_tok, head_tok, head_tok, head_tok],
        out_shape=[jax.ShapeDtypeStruct((D, T), BF16),
                   jax.ShapeDtypeStruct((ph, nk, T), BF16), jax.ShapeDtypeStruct((ph, nk, T), BF16),
                   jax.ShapeDtypeStruct((ph, nk, T), F32), jax.ShapeDtypeStruct((ph, nk, T), F32)],
        compiler_params=_params("parallel"),
    )(x1, norm2_gain[0].reshape(1, D), peer_w_query[0].T.astype(BF16), keys)

    tb = min(PEER_TOKENS, T)
    n_exp = peer_down.shape[1]
    step = PEER_PAIRS_PER_STEP * 2 * nk
    rows_per_step = PEER_PAIRS_PER_STEP * 2
    out = pl.pallas_call(
        _peer_kernel,
        name="peer_experts",
        grid=(T // tb, n_exp // step),
        in_specs=[pl.BlockSpec((D, tb), lambda t, s: (0, t)),
                  pl.BlockSpec((step, D), lambda t, s: (s, 0)),
                  pl.BlockSpec((D, step), lambda t, s: (0, s)),
                  pl.BlockSpec((ph, nk, tb), lambda t, s: (0, 0, t)),
                  pl.BlockSpec((ph, nk, tb), lambda t, s: (0, 0, t)),
                  pl.BlockSpec((ph, rows_per_step, tb), lambda t, s: (0, s, t)),
                  pl.BlockSpec((ph, rows_per_step, tb), lambda t, s: (0, s, t)),
                  pl.BlockSpec((tb, D), lambda t, s: (t, 0))],
        out_specs=pl.BlockSpec((tb, D), lambda t, s: (t, 0)),
        out_shape=jax.ShapeDtypeStruct((T, D), F32),
        scratch_shapes=[pltpu.VMEM((D, tb), F32)],
        compiler_params=_params("parallel", "arbitrary"),
    )(xnT, peer_down[0].astype(BF16), peer_up[0].T.astype(BF16), rank, e1, nkey, fgate, x1)
    return out


def kernel(x, positions, norm1_gain, w_in, ml_i_bias, ml_f_bias, q_norm_gain, k_norm_gain,
           attn_sinks, ml_out_norm_gain, w_branch_attn, w_branch_mlstm, w_out, norm2_gain,
           peer_w_query, peer_sub_keys, peer_down, peer_up):
    x1 = _token_mixer(x, positions, norm1_gain, w_in, ml_i_bias, ml_f_bias, q_norm_gain,
                      k_norm_gain, attn_sinks, ml_out_norm_gain, w_branch_attn, w_branch_mlstm,
                      w_out)
    out = _peer_ffn(x1, norm2_gain, peer_w_query, peer_sub_keys, peer_down, peer_up)
    return out.reshape(x.shape)
```

```python
import functools

import jax
import jax.numpy as jnp
import numpy as np
from jax import lax
from jax.experimental import pallas as pl
from jax.experimental.pallas import tpu as pltpu

F32 = jnp.float32
BF16 = jnp.bfloat16

NORM_EPS = 1e-6
D_MODEL_ = 1024
ATT_HEADS_ = 16
ATT_KV_HEADS_ = 2
ATT_DIM = 64
ATT_WINDOW = 128
ROPE_DIMS = 16
ROPE_BASE = 500000.0
ML_HEADS_ = 8
ML_QK = 64
ML_V = 128
GATE_SOFTCAP = 15.0
PEER_HEADS_ = 8
PEER_SUBKEY_DIM = 64
PEER_KEYS = 128
PEER_K = 16

LANES = 128
VMEM_LIMIT = 56 * 1024 * 1024

PROJ_TOKENS = 512
ML_CHUNK_LEN = 128
ROUTE_TOKENS = 256
PEER_TOKENS = 1024
PEER_PAIRS_PER_STEP = 4


def _params(*sem):
    return pltpu.CompilerParams(dimension_semantics=sem, vmem_limit_bytes=VMEM_LIMIT)


def _rms_rows(x, gain):
    r = lax.rsqrt(jnp.mean(x * x, axis=-1, keepdims=True) + NORM_EPS)
    return x * r * gain


def _trig_kernel(pos_ref, freq_ref, cos_ref, sin_ref):
    pos = pos_ref[...].astype(F32)
    for k in range(ROPE_DIMS // 2):
        ang = pos * freq_ref[k]
        cos_ref[k] = jnp.cos(ang)
        sin_ref[k] = jnp.sin(ang)


def _rope_tables(positions):
    t = positions.size
    half = ROPE_DIMS // 2
    inv_freq = ROPE_BASE ** (-jnp.arange(0, ROPE_DIMS, 2, dtype=F32) / ROPE_DIMS)
    pos2 = positions.reshape(t // LANES, LANES)
    rows = t // LANES
    cos, sin = pl.pallas_call(
        _trig_kernel,
        name="rope_trig",
        grid=(1,),
        in_specs=[pl.BlockSpec((rows, LANES), lambda i: (0, 0)),
                  pl.BlockSpec(memory_space=pltpu.SMEM)],
        out_specs=[pl.BlockSpec((half, rows, LANES), lambda i: (0, 0, 0))] * 2,
        out_shape=[jax.ShapeDtypeStruct((half, rows, LANES), F32)] * 2,
        compiler_params=_params("arbitrary"),
    )(pos2, inv_freq)
    cos = cos.reshape(half, t).T
    sin = sin.reshape(half, t).T
    ones = jnp.ones((t, ATT_DIM - ROPE_DIMS), F32)
    zeros8 = jnp.zeros((t, half), F32)
    zeros48 = jnp.zeros((t, ATT_DIM - ROPE_DIMS), F32)
    c_tab = jnp.concatenate([cos, cos, ones], axis=1)
    s_up = jnp.concatenate([-sin, zeros8, zeros48], axis=1)
    s_dn = jnp.concatenate([zeros8, sin, zeros48], axis=1)
    rep = LANES // ATT_DIM
    return tuple(jnp.tile(a, (1, rep)) for a in (c_tab, s_up, s_dn))


def _head_norm_rope(t, bsum, gain, c_tab, s_up, s_dn, scale):
    ms = jnp.dot((t * t).astype(BF16), bsum, preferred_element_type=F32)
    tn = t * lax.rsqrt(ms + NORM_EPS) * gain
    half = ROPE_DIMS // 2
    up = pltpu.roll(tn, LANES - half, axis=1)
    dn = pltpu.roll(tn, half, axis=1)
    return (tn * c_tab + up * s_up + dn * s_dn) * scale


def _attn_proj_kernel(x_ref, g1_ref, w_ref, bsum_ref, gq_ref, gk_ref, c_ref, su_ref, sd_ref,
                      q_out, k_out, v_out):
    h = _rms_rows(x_ref[...], g1_ref[...]).astype(BF16)
    p = jnp.dot(h, w_ref[...], preferred_element_type=F32)
    bsum = bsum_ref[...]
    c_tab, s_up, s_dn = c_ref[...], su_ref[...], sd_ref[...]
    qw = ATT_HEADS_ * ATT_DIM
    for c in range(qw // LANES):
        sl = slice(c * LANES, (c + 1) * LANES)
        q_out[:, sl] = _head_norm_rope(p[:, sl], bsum, gq_ref[...], c_tab, s_up, s_dn,
                                       ATT_DIM ** -0.5).astype(BF16)
    kw = ATT_KV_HEADS_ * ATT_DIM
    k_out[...] = _head_norm_rope(p[:, qw:qw + kw], bsum, gk_ref[...], c_tab, s_up, s_dn,
                                 1.0).astype(BF16)
    v_out[...] = p[:, qw + kw:qw + 2 * kw].astype(BF16)


def _soft_gates(z, is_input_gate):
    capped = GATE_SOFTCAP * jnp.tanh(z / GATE_SOFTCAP)
    return jnp.where(is_input_gate, capped, jax.nn.log_sigmoid(capped))


def _mlstm_proj_kernel(x_ref, g1_ref, wq_ref, wkT_ref, wv_ref, wg_ref, wgT_ref, bg_ref, bgT_ref,
                       q_out, kT_out, v_out, gcol_out, grow_out):
    h = _rms_rows(x_ref[...], g1_ref[...]).astype(BF16)
    q_out[...] = jnp.dot(h, wq_ref[...], preferred_element_type=F32).astype(BF16)
    v_out[...] = jnp.dot(h, wv_ref[...], preferred_element_type=F32).astype(BF16)
    nt = (((1,), (1,)), ((), ()))
    kT = lax.dot_general(wkT_ref[...], h, nt, preferred_element_type=F32)
    kT_out[...] = (kT * (ML_QK ** -0.5)).astype(BF16)
    zc = jnp.dot(h, wg_ref[...], preferred_element_type=F32) + bg_ref[...]
    col = lax.broadcasted_iota(jnp.int32, zc.shape, 1)
    gcol_out[...] = _soft_gates(zc, col < ML_HEADS_)
    zr = lax.dot_general(wgT_ref[...], h, nt, preferred_element_type=F32) + bgT_ref[...]
    row = lax.broadcasted_iota(jnp.int32, zr.shape, 0)
    grow_out[...] = _soft_gates(zr, row < ML_HEADS_)


def _gate_proj_kernel(x_ref, g1_ref, w_ref, o_out, ga_out, gm_out):
    h = _rms_rows(x_ref[...], g1_ref[...]).astype(BF16)
    d = D_MODEL_
    for i, out in enumerate((o_out, ga_out, gm_out)):
        z = jnp.dot(h, w_ref[:, i * d:(i + 1) * d], preferred_element_type=F32)
        out[...] = jax.nn.sigmoid(z).astype(BF16)


def _tok_spec(tm, width):
    return pl.BlockSpec((tm, width), lambda i: (i, 0))


def _whole(shape):
    nd = len(shape)
    return pl.BlockSpec(shape, lambda i: (0,) * nd)


def _attn_kernel(sink_ref, q_ref, kp_ref, kc_ref, vp_ref, vc_ref, o_ref):
    j = pl.program_id(1)
    w = ATT_WINDOW
    kk = jnp.concatenate([kp_ref[...], kc_ref[...]], axis=0)
    vv = jnp.concatenate([vp_ref[...], vc_ref[...]], axis=0)
    a = lax.broadcasted_iota(jnp.int32, (w, 2 * w), 0)
    c = lax.broadcasted_iota(jnp.int32, (w, 2 * w), 1)
    valid = (c > a) & (c <= a + w) & ((c >= w) | (j > 0))
    group = ATT_HEADS_ // ATT_KV_HEADS_
    nt = (((1,), (1,)), ((), ()))
    for h in range(ATT_HEADS_):
        g = h // group
        qh = q_ref[:, h * ATT_DIM:(h + 1) * ATT_DIM]
        kh = kk[:, g * ATT_DIM:(g + 1) * ATT_DIM]
        vh = vv[:, g * ATT_DIM:(g + 1) * ATT_DIM]
        s = lax.dot_general(qh, kh, nt, preferred_element_type=F32)
        s = jnp.where(valid, s, -jnp.inf)
        sink = sink_ref[h]
        m = jnp.maximum(jnp.max(s, axis=-1, keepdims=True), sink)
        p = jnp.exp(s - m)
        denom = jnp.sum(p, axis=-1, keepdims=True) + jnp.exp(sink - m)
        ph = (p / denom).astype(BF16)
        o_ref[:, h * ATT_DIM:(h + 1) * ATT_DIM] = jnp.dot(
            ph, vh, preferred_element_type=F32).astype(BF16)


def _mlstm_kernel(q_ref, kT_ref, v_ref, gcol_ref, grow_ref, og_ref, gain_ref, o_ref,
                  state_ref, m_ref):
    c = pl.program_id(1)
    L = ML_CHUNK_LEN

    @pl.when(c == 0)
    def _():
        state_ref[...] = jnp.zeros_like(state_ref)
        m_ref[...] = jnp.zeros_like(m_ref)

    t_idx = lax.broadcasted_iota(jnp.int32, (L, L), 0)
    s_idx = lax.broadcasted_iota(jnp.int32, (L, L), 1)
    causal = s_idx <= t_idx
    tri = causal.astype(F32)
    gcol = gcol_ref[...]
    grow = grow_ref[...]
    hi = lax.Precision.HIGHEST
    bcol = jnp.dot(tri, gcol, precision=hi, preferred_element_type=F32)
    brow = lax.dot_general(grow, tri, (((1,), (1,)), ((), ())), precision=hi,
                           preferred_element_type=F32)
    ones_col = (lax.broadcasted_iota(jnp.int32, (L, ML_V), 1) == 0).astype(BF16)
    nh = ML_HEADS_
    for h in range(nh):
        qh = q_ref[:, h * ML_QK:(h + 1) * ML_QK]
        kTh = kT_ref[h * ML_QK:(h + 1) * ML_QK, :]
        v_aug = jnp.concatenate([v_ref[:, h * ML_V:(h + 1) * ML_V], ones_col], axis=1)
        b_c = bcol[:, nh + h:nh + h + 1]
        b_r = brow[nh + h:nh + h + 1, :]
        i_r = grow[h:h + 1, :]
        m_prev = m_ref[h]
        state = state_ref[h]

        dmat = jnp.where(causal, b_c - b_r + i_r, -jnp.inf)
        inter = b_c + m_prev
        m_t = jnp.maximum(inter, jnp.max(dmat, axis=-1, keepdims=True))
        w_intra = jnp.exp(dmat - m_t)
        w_inter = jnp.exp(inter - m_t)
        qk = jnp.dot(qh, kTh, preferred_element_type=F32) * w_intra
        acc = jnp.dot(qk.astype(BF16), v_aug, preferred_element_type=F32)
        acc = acc + w_inter * jnp.dot(qh, state.astype(BF16), preferred_element_type=F32)
        num = acc[:, :ML_V]
        den = acc[:, ML_V:ML_V + 1]
        hh = num / jnp.maximum(jnp.abs(den), jnp.exp(-m_t))

        b_last = b_r[:, L - 1:L]
        g_r = b_last - b_r + i_r
        m_new = jnp.maximum(b_last + m_prev, jnp.max(g_r, axis=-1, keepdims=True))
        w_s = jnp.exp(g_r - m_new)
        decay = jnp.exp(b_last + m_prev - m_new)
        kw = (kTh.astype(F32) * w_s).astype(BF16)
        state_ref[h] = decay * state + jnp.dot(kw, v_aug, preferred_element_type=F32)
        m_ref[h] = m_new

        sl = slice(h * ML_V, (h + 1) * ML_V)
        hn = _rms_rows(hh, gain_ref[:, sl])
        o_ref[:, sl] = (hn * og_ref[:, sl].astype(F32)).astype(BF16)


def _merge_kernel(x_ref, att_ref, hm_ref, ga_ref, gm_ref, wa_ref, wm_ref, wo_ref, o_ref):
    att_b = jnp.dot(att_ref[...], wa_ref[...], preferred_element_type=F32)
    ml_b = jnp.dot(hm_ref[...], wm_ref[...], preferred_element_type=F32)
    mixed = ga_ref[...].astype(F32) * att_b + gm_ref[...].astype(F32) * ml_b
    o_ref[...] = x_ref[...] + jnp.dot(mixed.astype(BF16), wo_ref[...], preferred_element_type=F32)


def _top_values(s, count):
    rows = []
    cur = s
    for _ in range(count):
        m = jnp.max(cur, axis=0, keepdims=True)
        rows.append(m)
        cur = jnp.where(cur == m, -jnp.inf, cur)
    return jnp.concatenate(rows, axis=0)


def _route_kernel(x_ref, g2_ref, wqT_ref, keys_ref, xnT_out, rank_out, e1_out, nkey_out, f_out):
    xn = _rms_rows(x_ref[...], g2_ref[...])
    xnT = xn.T.astype(BF16)
    xnT_out[...] = xnT
    qT = jnp.dot(wqT_ref[...], xnT, preferred_element_type=F32).astype(BF16)
    k = PEER_K
    for h in range(PEER_HEADS_):
        s = []
        v = []
        for p in range(2):
            r0 = (h * 2 + p) * PEER_SUBKEY_DIM
            sp = jnp.dot(keys_ref[h * 2 + p], qT[r0:r0 + PEER_SUBKEY_DIM, :],
                         preferred_element_type=F32)
            s.append(sp)
            v.append(_top_values(sp, k))
        s0, s1 = s
        v0, v1 = v
        cand = [v0[0:1] + v1]
        cand += [v0[a:a + 1] + v1[0:8] for a in range(1, 8)]
        cand += [v0[8:16] + v1[0:1]]
        cand = jnp.concatenate(cand, axis=0)
        theta = _top_values(cand, k)[k - 1:k]
        top = v0[0:1] + v1[0:1]
        z = jnp.sum(jnp.where(cand >= theta, jnp.exp(cand - top), 0.0), axis=0, keepdims=True)
        nkey = jnp.zeros_like(s0)
        rank = jnp.zeros_like(s1)
        for b in range(k):
            nkey = nkey + jnp.where(s0 + v1[b:b + 1] >= theta, 1.0, 0.0)
            rank = rank + jnp.where(v1[b:b + 1] > s1, 1.0, 0.0)
        rank_out[h] = rank.astype(BF16)
        e1_out[h] = jnp.exp(s1 - v1[0:1]).astype(BF16)
        nkey_out[h] = nkey
        f_out[h] = jnp.exp(s0 - v0[0:1]) / z


def _peer_kernel(steps_per_tile, xnT_ref, down_ref, upT_ref, rank_ref, e1_ref, nkey_ref, f_ref,
                 x_ref, o_ref, acc_ref, w_ref):
    g = pl.program_id(0)
    lag_step = lax.rem(g + steps_per_tile - 1, steps_per_tile)
    slot = lax.rem(g, 2)
    tb = acc_ref.shape[1]

    @pl.when(g == 0)
    def _():
        w_ref[1] = jnp.zeros(w_ref.shape[1:], w_ref.dtype)

    @pl.when((lag_step == 0) | (g == 0))
    def _():
        acc_ref[...] = jnp.zeros_like(acc_ref)

    acc_ref[...] += jnp.dot(upT_ref[...], w_ref[1 - slot], preferred_element_type=F32)

    xnT = xnT_ref[...]
    blk = PEER_KEYS
    pack = 16
    zero = jnp.zeros((), BF16)
    for pr in range(PEER_PAIRS_PER_STEP):
        rows = slice(pr * 2 * blk, (pr + 1) * 2 * blk)
        s = jnp.dot(down_ref[rows, :], xnT, preferred_element_type=F32)
        act = (0.5 * s * (1.0 + lax.erf(s * np.float32(np.sqrt(0.5))))).astype(BF16)
        for r in range(2):
            row = pr * 2 + r
            gate = None
            for h in range(PEER_HEADS_):
                n = jnp.broadcast_to(nkey_ref[h, row:row + 1, :], (pack, tb)).astype(BF16)
                fr = jnp.broadcast_to(f_ref[h, row:row + 1, :], (pack, tb)).astype(BF16)
                term = jnp.where(rank_ref[h] < n[None], e1_ref[h], zero) * fr[None]
                gate = term if gate is None else gate + term
            lo = pr * 2 * blk + r * blk
            w_ref[slot, lo:lo + blk, :] = gate.reshape(blk, tb) * act[r * blk:(r + 1) * blk]

    @pl.when((lag_step == steps_per_tile - 1) & (g > 0))
    def _():
        o_ref[...] = x_ref[...] + acc_ref[...].T


def _token_mixer(x, positions, norm1_gain, w_in, ml_i_bias, ml_f_bias, q_norm_gain, k_norm_gain,
                 attn_sinks, ml_out_norm_gain, w_branch_attn, w_branch_mlstm, w_out):
    B, S, D = x.shape
    assert D == D_MODEL_ and norm1_gain.shape[0] == 1
    T = B * S
    tm = min(PROJ_TOKENS, T)
    assert T % tm == 0 and S % ATT_WINDOW == 0 and S % ML_CHUNK_LEN == 0
    x2 = x.reshape(T, D)
    g1 = norm1_gain[0].reshape(1, D)

    qw = ATT_HEADS_ * ATT_DIM
    kvw = ATT_KV_HEADS_ * ATT_DIM
    mqk = ML_HEADS_ * ML_QK
    mvw = ML_HEADS_ * ML_V
    nh = ML_HEADS_
    o0 = 0
    offs = {}
    for name, width in (("aq", qw), ("ak", kvw), ("av", kvw), ("mq", mqk), ("mk", mqk), ("mv", mvw),
                        ("mi", nh), ("mf", nh), ("mo", mvw), ("ga", D), ("gm", D)):
        offs[name] = (o0, o0 + width)
        o0 += width
    w = w_in[0]
    col = lambda a, b=None: w[:, offs[a][0]:offs[b or a][1]]

    c_tab, s_up, s_dn = _rope_tables(positions)
    seg = np.arange(qw) // ATT_DIM
    bsum = jnp.asarray((seg[:, None] == seg[None, :]).astype(np.float32) / ATT_DIM, BF16)
    bsum = bsum[:LANES, :LANES]
    rep = LANES // ATT_DIM
    gq = jnp.tile(q_norm_gain[0], rep).reshape(1, LANES)
    gk = jnp.tile(k_norm_gain[0], rep).reshape(1, LANES)
    w_att = col("aq", "av").astype(BF16)
    grid_t = (T // tm,)
    q, k, v = pl.pallas_call(
        _attn_proj_kernel,
        name="attn_proj",
        grid=grid_t,
        in_specs=[_tok_spec(tm, D), _whole((1, D)), _whole(w_att.shape), _whole(bsum.shape),
                  _whole((1, LANES)), _whole((1, LANES)),
                  _tok_spec(tm, LANES), _tok_spec(tm, LANES), _tok_spec(tm, LANES)],
        out_specs=[_tok_spec(tm, qw), _tok_spec(tm, kvw), _tok_spec(tm, kvw)],
        out_shape=[jax.ShapeDtypeStruct((T, qw), BF16), jax.ShapeDtypeStruct((T, kvw), BF16),
                   jax.ShapeDtypeStruct((T, kvw), BF16)],
        compiler_params=_params("parallel"),
    )(x2, g1, w_att, bsum, gq, gk, c_tab, s_up, s_dn)

    wg = jnp.zeros((D, LANES), F32).at[:, :2 * nh].set(col("mi", "mf")).astype(BF16)
    bg = jnp.zeros((1, LANES), F32).at[0, :nh].set(ml_i_bias[0]).at[0, nh:2 * nh].set(ml_f_bias[0])
    wgT = wg[:, :2 * nh].T
    bgT = bg[:, :2 * nh].T
    mq, mkT, mv, gcol, grow = pl.pallas_call(
        _mlstm_proj_kernel,
        name="mlstm_proj",
        grid=grid_t,
        in_specs=[_tok_spec(tm, D), _whole((1, D)), _whole((D, mqk)), _whole((mqk, D)),
                  _whole((D, mvw)), _whole((D, LANES)), _whole((2 * nh, D)), _whole((1, LANES)),
                  _whole((2 * nh, 1))],
        out_specs=[_tok_spec(tm, mqk), pl.BlockSpec((mqk, tm), lambda i: (0, i)),
                   _tok_spec(tm, mvw), _tok_spec(tm, LANES),
                   pl.BlockSpec((2 * nh, tm), lambda i: (0, i))],
        out_shape=[jax.ShapeDtypeStruct((T, mqk), BF16), jax.ShapeDtypeStruct((mqk, T), BF16),
                   jax.ShapeDtypeStruct((T, mvw), BF16), jax.ShapeDtypeStruct((T, LANES), F32),
                   jax.ShapeDtypeStruct((2 * nh, T), F32)],
        compiler_params=_params("parallel"),
    )(x2, g1, col("mq").astype(BF16), col("mk").T.astype(BF16), col("mv").astype(BF16),
      wg, wgT, bg, bgT)

    w_g3 = col("mo", "gm").astype(BF16)
    og, ga, gm = pl.pallas_call(
        _gate_proj_kernel,
        name="gate_proj",
        grid=grid_t,
        in_specs=[_tok_spec(tm, D), _whole((1, D)), _whole(w_g3.shape)],
        out_specs=[_tok_spec(tm, D)] * 3,
        out_shape=[jax.ShapeDtypeStruct((T, D), BF16)] * 3,
        compiler_params=_params("parallel"),
    )(x2, g1, w_g3)

    nb = S // ATT_WINDOW
    cur = lambda b, j: (b * nb + j, 0)
    prev = lambda b, j: (b * nb + jnp.maximum(j - 1, 0), 0)
    att = pl.pallas_call(
        _attn_kernel,
        name="swa_attention",
        grid=(B, nb),
        in_specs=[pl.BlockSpec(memory_space=pltpu.SMEM),
                  pl.BlockSpec((ATT_WINDOW, qw), cur),
                  pl.BlockSpec((ATT_WINDOW, kvw), prev), pl.BlockSpec((ATT_WINDOW, kvw), cur),
                  pl.BlockSpec((ATT_WINDOW, kvw), prev), pl.BlockSpec((ATT_WINDOW, kvw), cur)],
        out_specs=pl.BlockSpec((ATT_WINDOW, qw), cur),
        out_shape=jax.ShapeDtypeStruct((T, qw), BF16),
        compiler_params=_params("parallel", "parallel"),
    )(attn_sinks[0], q, k, k, v, v)

    L = ML_CHUNK_LEN
    nc = S // L
    tok = lambda b, c: (b * nc + c, 0)
    tokT = lambda b, c: (0, b * nc + c)
    hm = pl.pallas_call(
        _mlstm_kernel,
        name="mlstm_chunks",
        grid=(B, nc),
        in_specs=[pl.BlockSpec((L, mqk), tok), pl.BlockSpec((mqk, L), tokT),
                  pl.BlockSpec((L, mvw), tok), pl.BlockSpec((L, LANES), tok),
                  pl.BlockSpec((2 * nh, L), tokT), pl.BlockSpec((L, mvw), tok),
                  pl.BlockSpec((1, mvw), lambda b, c: (0, 0))],
        out_specs=pl.BlockSpec((L, mvw), tok),
        out_shape=jax.ShapeDtypeStruct((T, mvw), BF16),
        scratch_shapes=[pltpu.VMEM((nh, ML_QK, 2 * ML_V), F32), pltpu.VMEM((nh, 1, 1), F32)],
        compiler_params=_params("parallel", "arbitrary"),
    )(mq, mkT, mv, gcol, grow, og, ml_out_norm_gain[0].reshape(1, mvw))

    x1 = pl.pallas_call(
        _merge_kernel,
        name="gated_merge",
        grid=grid_t,
        in_specs=[_tok_spec(tm, D), _tok_spec(tm, qw), _tok_spec(tm, mvw), _tok_spec(tm, D),
                  _tok_spec(tm, D), _whole((qw, D)), _whole((mvw, D)), _whole((D, D))],
        out_specs=_tok_spec(tm, D),
        out_shape=jax.ShapeDtypeStruct((T, D), F32),
        compiler_params=_params("parallel"),
    )(x2, att, hm, ga, gm, w_branch_attn[0].astype(BF16), w_branch_mlstm[0].astype(BF16),
      w_out[0].astype(BF16))
    return x1


def _peer_ffn(x1, norm2_gain, peer_w_query, peer_sub_keys, peer_down, peer_up):
    T, D = x1.shape
    tr = min(ROUTE_TOKENS, T)
    ph = PEER_HEADS_
    nk = PEER_KEYS
    keys = peer_sub_keys[0].reshape(ph * 2, nk, PEER_SUBKEY_DIM).astype(BF16)
    head_tok = pl.BlockSpec((ph, nk, tr), lambda i: (0, 0, i))
    xnT, rank, e1, nkey, fgate = pl.pallas_call(
        _route_kernel,
        name="peer_route",
        grid=(T // tr,),
        in_specs=[_tok_spec(tr, D), _whole((1, D)), _whole((ph * 2 * PEER_SUBKEY_DIM, D)),
                  _whole(keys.shape)],
        out_specs=[pl.BlockSpec((D, tr), lambda i: (0, i)), head_tok, head_tok, head_tok, head_tok],
        out_shape=[jax.ShapeDtypeStruct((D, T), BF16),
                   jax.ShapeDtypeStruct((ph, nk, T), BF16), jax.ShapeDtypeStruct((ph, nk, T), BF16),
                   jax.ShapeDtypeStruct((ph, nk, T), F32), jax.ShapeDtypeStruct((ph, nk, T), F32)],
        compiler_params=_params("parallel"),
    )(x1, norm2_gain[0].reshape(1, D), peer_w_query[0].T.astype(BF16), keys)

    tb = min(PEER_TOKENS, T)
    n_exp = peer_down.shape[1]
    step = PEER_PAIRS_PER_STEP * 2 * nk
    rows_per_step = PEER_PAIRS_PER_STEP * 2
    spt = n_exp // step
    n_tiles = T // tb
    pack = 16
    tile = lambda g: jnp.minimum(g // spt, n_tiles - 1)
    lag_tile = lambda g: jnp.maximum(g - 1, 0) // spt
    lag_step = lambda g: lax.rem(jnp.maximum(g - 1, 0), spt)
    rank4 = rank.reshape(ph, nk // pack, pack, T)
    e14 = e1.reshape(ph, nk // pack, pack, T)
    out = pl.pallas_call(
        functools.partial(_peer_kernel, spt),
        name="peer_experts",
        grid=(n_tiles * spt + 1,),
        in_specs=[pl.BlockSpec((D, tb), lambda g: (0, tile(g))),
                  pl.BlockSpec((step, D), lambda g: (lax.rem(g, spt), 0)),
                  pl.BlockSpec((D, step), lambda g: (0, lag_step(g))),
                  pl.BlockSpec((ph, nk // pack, pack, tb), lambda g: (0, 0, 0, tile(g))),
                  pl.BlockSpec((ph, nk // pack, pack, tb), lambda g: (0, 0, 0, tile(g))),
                  pl.BlockSpec((ph, rows_per_step, tb), lambda g: (0, lax.rem(g, spt), tile(g))),
                  pl.BlockSpec((ph, rows_per_step, tb), lambda g: (0, lax.rem(g, spt), tile(g))),
                  pl.BlockSpec((tb, D), lambda g: (lag_tile(g), 0))],
        out_specs=pl.BlockSpec((tb, D), lambda g: (lag_tile(g), 0)),
        out_shape=jax.ShapeDtypeStruct((T, D), F32),
        scratch_shapes=[pltpu.VMEM((D, tb), F32), pltpu.VMEM((2, step, tb), BF16)],
        compiler_params=_params("arbitrary"),
    )(xnT, peer_down[0].astype(BF16), peer_up[0].T.astype(BF16), rank4, e14, nkey, fgate, x1)
    return out


def kernel(x, positions, norm1_gain, w_in, ml_i_bias, ml_f_bias, q_norm_gain, k_norm_gain,
           attn_sinks, ml_out_norm_gain, w_branch_attn, w_branch_mlstm, w_out, norm2_gain,
           peer_w_query, peer_sub_keys, peer_down, peer_up):
    x1 = _token_mixer(x, positions, norm1_gain, w_in, ml_i_bias, ml_f_bias, q_norm_gain,
                      k_norm_gain, attn_sinks, ml_out_norm_gain, w_branch_attn, w_branch_mlstm,
                      w_out)
    out = _peer_ffn(x1, norm2_gain, peer_w_query, peer_sub_keys, peer_down, peer_up)
    return out.reshape(x.shape)
```
